```python
import jax, jax.numpy as jnp
from jax import lax
import numpy as np

D_MODEL = 2048
BATCH = 2
SEQ = 16384
DEPTH = 1

CTX_LEN = 256
GRID_W = 64
MIX_WIDTH = D_MODEL
HG_WIDTH = D_MODEL // 2
HG_DK = 128
HG_DV = 128
HG_HEADS = HG_WIDTH // HG_DK
CONV_WIDTH = MIX_WIDTH - HG_WIDTH
CONV_K = 3
CHUNK = 64
D_FF = ((8 * D_MODEL // 3 + 255) // 256) * 256
N_IN = 5 * HG_WIDTH + 3 * CONV_WIDTH
ALPHA = (2.0 * DEPTH) ** 0.25
BETA = (8.0 * DEPTH) ** -0.25
LN_EPS = 1e-6
RMS_EPS = 1e-6

kernel_name = "hymba_hgrn2_shortconv_dit_layer"


def layer_norm(x, gain=None, bias=None):
    xf = x.astype(jnp.float32)
    mu = jnp.mean(xf, axis=-1, keepdims=True)
    var = jnp.mean(jnp.square(xf - mu), axis=-1, keepdims=True)
    y = (xf - mu) * lax.rsqrt(var + LN_EPS)
    if gain is not None:
        y = y * gain.astype(jnp.float32) + bias.astype(jnp.float32)
    return y.astype(x.dtype)


def modulate(x, shift, scale):
    return layer_norm(x) * (1 + scale) + shift


def rms_norm(x, w):
    xf = x.astype(jnp.float32)
    return xf * lax.rsqrt(jnp.mean(jnp.square(xf), axis=-1, keepdims=True) + RMS_EPS) * w.astype(jnp.float32)


def lower_bounds(lb_logits, l):
    p = jax.nn.softmax(lb_logits.astype(jnp.float32), axis=1)
    return jnp.cumsum(p, axis=1)[:, l]


def to_dirs(a, n_heads, d):
    B, N, _ = a.shape
    return jnp.stack([a, a[:, ::-1]]).reshape(2 * B, N, n_heads, d)


def hgrn2_kv(p3, lb):
    B, N, _ = p3.shape
    f_raw = p3[..., :2 * HG_WIDTH].astype(jnp.float32)
    f_fwd = lb[0] + (1 - lb[0]) * jax.nn.sigmoid(f_raw[..., :HG_WIDTH])
    f_bwd = lb[1] + (1 - lb[1]) * jax.nn.sigmoid(f_raw[..., HG_WIDTH:])
    f = jnp.stack([f_fwd, f_bwd[:, ::-1]]).reshape(2 * B, N, HG_HEADS, HG_DK)
    v = to_dirs(p3[..., 2 * HG_WIDTH:].astype(jnp.float32), HG_HEADS, HG_DV)
    return 1 - f, jnp.log(f), v


def chunked(a):
    G, N, H, d = a.shape
    return a.reshape(G, N // CHUNK, CHUNK, H, d).transpose(1, 0, 3, 2, 4)


def hgrn2_final_state(k, logf, v):
    G, _, H, _ = k.shape
    S0 = jnp.zeros((G, H, HG_DK, HG_DV), jnp.float32)

    def step(S, inp):
        kc, gc, vc = inp
        b = jnp.cumsum(gc, axis=2)
        b_end = b[:, :, -1:, :]
        S = jnp.swapaxes(jnp.exp(b_end), -1, -2) * S + jnp.einsum('ghsk,ghsv->ghkv', kc * jnp.exp(b_end - b), vc)
        return S, None

    S, _ = lax.scan(step, S0, (chunked(k), chunked(logf), chunked(v)))
    return S


def hgrn2_scan(q, k, logf, v, S0):
    G, N, H, _ = q.shape
    tril = jnp.tril(jnp.ones((CHUNK, CHUNK), dtype=bool))

    def step(S, inp):
        qc, kc, gc, vc = inp
        b = jnp.cumsum(gc, axis=2)
        diff = b[:, :, :, None, :] - b[:, :, None, :, :]
        decay = jnp.exp(jnp.where(tril[:, :, None], diff, -jnp.inf))
        scores = jnp.einsum('ghtk,ghsk,ghtsk->ghts', qc, kc, decay)
        o = jnp.einsum('ghts,ghsv->ghtv', scores, vc) + jnp.einsum('ghtk,ghkv->ghtv', qc * jnp.exp(b), S)
        b_end = b[:, :, -1:, :]
        S = jnp.swapaxes(jnp.exp(b_end), -1, -2) * S + jnp.einsum('ghsk,ghsv->ghkv', kc * jnp.exp(b_end - b), vc)
        return S, o

    S, o = lax.scan(step, S0, (chunked(q), chunked(k), chunked(logf), chunked(v)))
    o = o.transpose(1, 0, 3, 2, 4).reshape(G, N, H, HG_DV)
    return o, S


def dwconv3(z, w, n_segments):
    B, N, C = z.shape
    L = N // n_segments
    zp = jnp.pad(z.reshape(B, n_segments, L, C), ((0, 0), (0, 0), (1, 1), (0, 0)))
    y = w[0] * zp[:, :, :L] + w[1] * zp[:, :, 1:L + 1] + w[2] * zp[:, :, 2:]
    return y.reshape(B, N, C)


def mixer(proj, S0, lb, g_norm_w, conv_w, n_segments):
    B, N, _ = proj.shape
    k_d, logf_d, v_d = hgrn2_kv(proj[..., :3 * HG_WIDTH], lb)
    q_d = to_dirs(jax.nn.silu(proj[..., 3 * HG_WIDTH:4 * HG_WIDTH].astype(jnp.float32)), HG_HEADS, HG_DK)
    o_d, S = hgrn2_scan(q_d, k_d, logf_d, v_d, S0)
    o_d = o_d.reshape(2, B, N, HG_HEADS, HG_DV)
    o = o_d[0] + o_d[1][:, ::-1]
    gate = jax.nn.silu(proj[..., 4 * HG_WIDTH:5 * HG_WIDTH].astype(jnp.float32))
    y_hg = (rms_norm(o, g_norm_w).reshape(B, N, HG_WIDTH) * gate).astype(proj.dtype)
    b_gate, c_gate, xv = jnp.split(proj[..., 5 * HG_WIDTH:], 3, axis=-1)
    y_conv = b_gate * dwconv3(c_gate * xv, conv_w, n_segments)
    return jnp.concatenate([y_hg, y_conv], axis=-1), S


def swiglu(h, w_gate, w_up, w_down):
    return (jax.nn.silu(h @ w_gate) * (h @ w_up)) @ w_down


def trunk_layer(x, xc, c, c_ctx, n_rows, w_mod, b_mod, w_in, lb, g_norm_w, conv_w, w_out,
                ln1_g, ln1_b, w_gate, w_up, w_down, ln2_g, ln2_b, update_ctx):
    mod = jax.nn.silu(c) @ w_mod + b_mod
    mod_c = jax.nn.silu(c_ctx) @ w_mod + b_mod
    sh_a, sc_a, ga_a, sh_f, sc_f, ga_f = jnp.split(mod[:, None, :], 6, axis=-1)
    shc_a, scc_a, gac_a, shc_f, scc_f, gac_f = jnp.split(mod_c, 6, axis=-1)

    hc = modulate(xc, shc_a, scc_a)
    if update_ctx:
        B = xc.shape[0]
        zero_state = jnp.zeros((2 * B, HG_HEADS, HG_DK, HG_DV), jnp.float32)
        yc, S_ctx = mixer(hc @ w_in, zero_state, lb, g_norm_w, conv_w, 1)
        xc = layer_norm(ALPHA * xc + gac_a * (yc @ w_out), ln1_g, ln1_b)
        hc = modulate(xc, shc_f, scc_f)
        xc = layer_norm(ALPHA * xc + gac_f * swiglu(hc, w_gate, w_up, w_down), ln2_g, ln2_b)
    else:
        S_ctx = hgrn2_final_state(*hgrn2_kv(hc @ w_in[:, :3 * HG_WIDTH], lb))

    h = modulate(x, sh_a, sc_a)
    y, _ = mixer(h @ w_in, S_ctx, lb, g_norm_w, conv_w, n_rows)
    x = layer_norm(ALPHA * x + ga_a * (y @ w_out), ln1_g, ln1_b)
    h = modulate(x, sh_f, sc_f)
    x = layer_norm(ALPHA * x + ga_f * swiglu(h, w_gate, w_up, w_down), ln2_g, ln2_b)
    return x, xc


def setup_inputs(seed: int = 0) -> dict:
    key = jax.random.key(seed)
    ks = jax.random.split(key, 20)
    D = D_MODEL
    nrm = jax.random.normal
    return {
        "x": nrm(ks[0], (BATCH, SEQ, D), jnp.float32),
        "c": nrm(ks[1], (BATCH, D), jnp.float32),
        "ctx": nrm(ks[2], (BATCH, CTX_LEN, D), jnp.float32),
        "c_ctx": nrm(ks[3], (D,), jnp.float32),
        "w_mod": nrm(ks[4], (DEPTH, D, 6 * D), jnp.float32) * (0.5 * D ** -0.5),
        "b_mod": nrm(ks[5], (DEPTH, 6 * D), jnp.float32) * 0.01,
        "w_in": nrm(ks[6], (DEPTH, D, N_IN), jnp.float32) * D ** -0.5,
        "lb_logits": nrm(ks[7], (2, DEPTH + 1, HG_WIDTH), jnp.float32) * 0.5,
        "g_norm_w": 1.0 + 0.01 * nrm(ks[8], (DEPTH, HG_DV), jnp.float32),
        "conv_w": nrm(ks[9], (DEPTH, CONV_K, CONV_WIDTH), jnp.float32) * CONV_K ** -0.5,
        "w_out": nrm(ks[10], (DEPTH, MIX_WIDTH, D), jnp.float32) * (BETA * MIX_WIDTH ** -0.5),
        "ln1_g": 1.0 + 0.01 * nrm(ks[11], (DEPTH, D), jnp.float32),
        "ln1_b": 0.01 * nrm(ks[12], (DEPTH, D), jnp.float32),
        "w_gate": nrm(ks[13], (DEPTH, D, D_FF), jnp.float32) * D ** -0.5,
        "w_up": nrm(ks[14], (DEPTH, D, D_FF), jnp.float32) * D ** -0.5,
        "w_down": nrm(ks[15], (DEPTH, D_FF, D), jnp.float32) * (BETA * D_FF ** -0.5),
        "ln2_g": 1.0 + 0.01 * nrm(ks[16], (DEPTH, D), jnp.float32),
        "ln2_b": 0.01 * nrm(ks[17], (DEPTH, D), jnp.float32),
    }


def reference(x, c, ctx, c_ctx, w_mod, b_mod, w_in, lb_logits, g_norm_w, conv_w, w_out,
              ln1_g, ln1_b, w_gate, w_up, w_down, ln2_g, ln2_b):
    n_rows = x.shape[1] // GRID_W
    xc = ctx
    for l in range(DEPTH):
        lb = lower_bounds(lb_logits, l)
        x, xc = trunk_layer(x, xc, c, c_ctx, n_rows, w_mod[l], b_mod[l], w_in[l], lb, g_norm_w[l],
                            conv_w[l], w_out[l], ln1_g[l], ln1_b[l], w_gate[l], w_up[l], w_down[l],
                            ln2_g[l], ln2_b[l], l < DEPTH - 1)
    return x
```

```python
import functools

import jax
import jax.numpy as jnp
from jax import lax
from jax.experimental import pallas as pl
from jax.experimental.pallas import tpu as pltpu

HEAD = 128
HG_WIDTH = 1024
N_HEADS = HG_WIDTH // HEAD
CONV_WIDTH = 1024
GRID_W = 64
DEPTH = 1
ALPHA = (2.0 * DEPTH) ** 0.25
LN_EPS = 1e-6
RMS_EPS = 1e-6

CHUNK = 64
EXP_CLAMP = 80.0
ROWS = 64
VMEM_LIMIT = 56 * 1024 * 1024

F32 = jnp.float32
BF16 = jnp.bfloat16
HIGHEST = lax.Precision.HIGHEST


def _sigmoid(x):
    return 1.0 / (1.0 + jnp.exp(-x))


def _ln(x):
    mu = jnp.mean(x, axis=-1, keepdims=True)
    xc = x - mu
    var = jnp.mean(xc * xc, axis=-1, keepdims=True)
    return xc * lax.rsqrt(var + LN_EPS)


def _params(*sem):
    return pltpu.CompilerParams(dimension_semantics=sem, vmem_limit_bytes=VMEM_LIMIT)


def _mod_kernel(a_ref, w_ref, b_ref, o_ref):
    a = a_ref[...]
    a = a * _sigmoid(a)
    o_ref[...] = jnp.dot(a, w_ref[...], precision=HIGHEST, preferred_element_type=F32) + b_ref[...]


def _mod(a, w, b):
    d, n = w.shape
    tn = 1024
    return pl.pallas_call(
        _mod_kernel,
        grid=(n // tn,),
        in_specs=[
            pl.BlockSpec((8, d), lambda j: (0, 0)),
            pl.BlockSpec((d, tn), lambda j: (0, j)),
            pl.BlockSpec((1, tn), lambda j: (0, j)),
        ],
        out_specs=pl.BlockSpec((8, tn), lambda j: (0, j)),
        out_shape=jax.ShapeDtypeStruct((8, n), F32),
        compiler_params=_params("parallel"),
        name="mod",
    )(a, w, b)


def _ln_matmul_kernel(x_ref, sh_ref, sc_ref, w_ref, o_ref, h_ref):
    tm = x_ref.shape[0]

    @pl.when(pl.program_id(1) == 0)
    def _():
        scale = 1.0 + sc_ref[0]
        shift = sh_ref[0]

        def body(r, carry):
            rows = pl.ds(pl.multiple_of(r * ROWS, ROWS), ROWS)
            h_ref[rows, :] = (_ln(x_ref[rows, :]) * scale + shift).astype(BF16)
            return carry

        lax.fori_loop(0, tm // ROWS, body, 0)

    o_ref[...] = jnp.dot(h_ref[...], w_ref[...], preferred_element_type=F32).astype(o_ref.dtype)


def _ln_matmul(x2, shift, scale, w, rows_per_mod, tm, tn, out_dtype):
    r, d = x2.shape
    n = w.shape[1]
    tiles_per_mod = rows_per_mod // tm
    mod_map = lambda i, j: (i // tiles_per_mod, 0, 0)
    return pl.pallas_call(
        _ln_matmul_kernel,
        grid=(r // tm, n // tn),
        in_specs=[
            pl.BlockSpec((tm, d), lambda i, j: (i, 0)),
            pl.BlockSpec((1, 1, d), mod_map),
            pl.BlockSpec((1, 1, d), mod_map),
            pl.BlockSpec((d, tn), lambda i, j: (0, j)),
        ],
        out_specs=pl.BlockSpec((tm, tn), lambda i, j: (i, j)),
        out_shape=jax.ShapeDtypeStruct((r, n), out_dtype),
        scratch_shapes=[pltpu.VMEM((tm, d), BF16)],
        compiler_params=_params("parallel", "arbitrary"),
        name="ln_matmul",
    )(x2, shift, scale, w)


def _lower_bound(lbl, direction):
    return _sigmoid(lbl[2 * direction:2 * direction + 1, :] - lbl[2 * direction + 1:2 * direction + 2, :])


def _forget(f_raw, lb):
    f = lb + (1.0 - lb) * _sigmoid(f_raw)
    return 1.0 - f, jnp.log(f)


def _tri(n, kind):
    r = lax.broadcasted_iota(jnp.int32, (n, n), 0)
    c = lax.broadcasted_iota(jnp.int32, (n, n), 1)
    cond = {"le": c <= r, "ge": c >= r, "lt": c < r, "gt": c > r}[kind]
    return cond


def _ctx_state_kernel(pf_ref, pb_ref, pi_ref, lbl_ref, sf_ref, sb_ref):
    n = pf_ref.shape[1]
    lbl = lbl_ref[...]
    v = pi_ref[0]
    for d, (p_ref, kind, s_ref) in enumerate(((pf_ref, "gt", sf_ref), (pb_ref, "lt", sb_ref))):
        k, lg = _forget(p_ref[0], _lower_bound(lbl, d))
        tri = _tri(n, kind).astype(F32)
        w = jnp.exp(jnp.dot(tri, lg, precision=HIGHEST, preferred_element_type=F32))
        s_ref[0, 0] = lax.dot_general(v, k * w, (((0,), (0,)), ((), ())), precision=HIGHEST,
                                      preferred_element_type=F32)


def _ctx_state(pc3, lbl):
    b, n, _ = pc3.shape
    hb = HG_WIDTH // HEAD
    spec = lambda off: pl.BlockSpec((1, n, HEAD), lambda bi, h: (bi, 0, off * hb + h))
    s_spec = pl.BlockSpec((1, 1, HEAD, HEAD), lambda bi, h: (bi, h, 0, 0))
    s_shape = jax.ShapeDtypeStruct((b, N_HEADS, HEAD, HEAD), F32)
    return pl.pallas_call(
        _ctx_state_kernel,
        grid=(b, N_HEADS),
        in_specs=[spec(0), spec(1), spec(2), pl.BlockSpec((4, HEAD), lambda bi, h: (0, h))],
        out_specs=[s_spec, s_spec],
        out_shape=[s_shape, s_shape],
        compiler_params=_params("parallel", "parallel"),
        name="ctx_state",
    )(pc3, pc3, pc3, lbl)


def _scan_chunk(f_raw, v, q_raw, lb, st, tri, mask, end_row):
    c = f_raw.shape[0]
    k, lg = _forget(f_raw, lb)
    q = q_raw * _sigmoid(q_raw)
    b = jnp.dot(tri, lg, precision=HIGHEST, preferred_element_type=F32)
    b_end = b[end_row:end_row + 1, :]
    m = 0.5 * b_end
    qt = (q * jnp.exp(jnp.minimum(b - m, EXP_CLAMP))).astype(BF16)
    kt = (k * jnp.exp(jnp.minimum(m - b, EXP_CLAMP))).astype(BF16)
    w = jnp.concatenate([(st * jnp.exp(m)).astype(BF16), kt], axis=0)
    a = lax.dot_general(qt, w, (((1,), (1,)), ((), ())), preferred_element_type=F32)
    p = jnp.where(mask, a[:, HEAD:HEAD + c], 0.0).astype(BF16)
    vb = v.astype(BF16)
    o = a[:, :HEAD] + jnp.dot(p, vb, preferred_element_type=F32)
    kp = (k * jnp.exp(b_end - b)).astype(BF16)
    st_new = st * jnp.exp(b_end) + lax.dot_general(vb, kp, (((0,), (0,)), ((), ())),
                                                   preferred_element_type=F32)
    return o, st_new


def _scan_kernel(ff_ref, if_ref, qf_ref, fb_ref, ib_ref, qb_ref, lbl_ref, sf0_ref, sb0_ref,
                 of_ref, ob_ref, stf_ref, stb_ref):
    t = ff_ref.shape[1]
    hb = ff_ref.shape[2] // HEAD
    nc = t // CHUNK

    @pl.when(pl.program_id(2) == 0)
    def _():
        stf_ref[...] = sf0_ref[0]
        stb_ref[...] = sb0_ref[0]

    lbl = lbl_ref[...]
    lb_f = _lower_bound(lbl, 0)
    lb_b = _lower_bound(lbl, 1)
    tri_f = _tri(CHUNK, "le").astype(F32)
    tri_b = _tri(CHUNK, "ge").astype(F32)
    mask_f = _tri(CHUNK, "le")
    mask_b = _tri(CHUNK, "ge")

    def body(j, carry):
        rf = pl.ds(pl.multiple_of(j * CHUNK, CHUNK), CHUNK)
        rb = pl.ds(pl.multiple_of((nc - 1 - j) * CHUNK, CHUNK), CHUNK)
        for h in range(hb):
            cols = slice(h * HEAD, (h + 1) * HEAD)
            o, st = _scan_chunk(ff_ref[0, rf, cols], if_ref[0, rf, cols], qf_ref[0, rf, cols],
                                lb_f[:, cols], stf_ref[h], tri_f, mask_f, CHUNK - 1)
            of_ref[0, rf, cols] = o
            stf_ref[h] = st
            o, st = _scan_chunk(fb_ref[0, rb, cols], ib_ref[0, rb, cols], qb_ref[0, rb, cols],
                                lb_b[:, cols], stb_ref[h], tri_b, mask_b, 0)
            ob_ref[0, rb, cols] = o
            stb_ref[h] = st
        return carry

    lax.fori_loop(0, nc, body, 0)


def _scan(proj3, lbl, sf0, sb0, t, hb):
    b, n, _ = proj3.shape
    nt = n // t
    w = hb * HEAD
    grp = HG_WIDTH // w
    fwd = lambda g: pl.BlockSpec((1, t, w), lambda bi, h, c: (bi, c, g * grp + h))
    bwd = lambda g: pl.BlockSpec((1, t, w), lambda bi, h, c: (bi, nt - 1 - c, g * grp + h))
    s_spec = pl.BlockSpec((1, hb, HEAD, HEAD), lambda bi, h, c: (bi, h, 0, 0))
    o_shape = jax.ShapeDtypeStruct((b, n, HG_WIDTH), F32)
    return pl.pallas_call(
        _scan_kernel,
        grid=(b, N_HEADS // hb, nt),
        in_specs=[fwd(0), fwd(2), fwd(3), bwd(1), bwd(2), bwd(3),
                  pl.BlockSpec((4, w), lambda bi, h, c: (0, h)), s_spec, s_spec],
        out_specs=[pl.BlockSpec((1, t, w), lambda bi, h, c: (bi, c, h)),
                   pl.BlockSpec((1, t, w), lambda bi, h, c: (bi, nt - 1 - c, h))],
        out_shape=[o_shape, o_shape],
        scratch_shapes=[pltpu.VMEM((hb, HEAD, HEAD), F32), pltpu.VMEM((hb, HEAD, HEAD), F32)],
        compiler_params=_params("parallel", "parallel", "arbitrary"),
        name="scan",
    )(proj3, proj3, proj3, proj3, proj3, proj3, lbl, sf0, sb0)


def _mix_out_kernel(of_ref, ob_ref, g_ref, bg_ref, cg_ref, xv_ref, x_ref, gnw_ref, cw_ref, wo_ref,
                    ga_ref, l1g_ref, l1b_ref, shf_ref, scf_ref, x1_ref, h2_ref, y_ref):
    tm = x_ref.shape[0]
    gnw = gnw_ref[...]
    cw = cw_ref[...]
    row = lax.broadcasted_iota(jnp.int32, (GRID_W, 1), 0)
    first = row == 0
    last = row == GRID_W - 1

    def prologue(r, carry):
        rows = pl.ds(pl.multiple_of(r * GRID_W, GRID_W), GRID_W)
        for h in range(N_HEADS):
            cols = slice(h * HEAD, (h + 1) * HEAD)
            o = of_ref[rows, cols] + ob_ref[rows, cols]
            ms = jnp.mean(o * o, axis=-1, keepdims=True)
            g = g_ref[rows, cols]
            y_ref[rows, cols] = (o * lax.rsqrt(ms + RMS_EPS) * gnw * (g * _sigmoid(g))).astype(BF16)
        z = cg_ref[rows, :] * xv_ref[rows, :]
        z_prev = jnp.where(first, 0.0, pltpu.roll(z, 1, 0))
        z_next = jnp.where(last, 0.0, pltpu.roll(z, GRID_W - 1, 0))
        conv = cw[0:1, :] * z_prev + cw[1:2, :] * z + cw[2:3, :] * z_next
        y_ref[rows, HG_WIDTH:] = (bg_ref[rows, :] * conv).astype(BF16)
        return carry

    lax.fori_loop(0, tm // GRID_W, prologue, 0)

    y_ref_f32 = jnp.dot(y_ref[...], wo_ref[...], preferred_element_type=F32)
    x1_ref[...] = y_ref_f32

    ga = ga_ref[0]
    l1g = l1g_ref[...]
    l1b = l1b_ref[...]
    scale = 1.0 + scf_ref[0]
    shift = shf_ref[0]

    def epilogue(r, carry):
        rows = pl.ds(pl.multiple_of(r * ROWS, ROWS), ROWS)
        x1 = _ln(ALPHA * x_ref[rows, :] + ga * x1_ref[rows, :]) * l1g + l1b
        x1_ref[rows, :] = x1
        h2_ref[rows, :] = (_ln(x1) * scale + shift).astype(BF16)
        return carry

    lax.fori_loop(0, tm // ROWS, epilogue, 0)


def _mix_out(of2, ob2, proj2, x2, gnw, cw, wo, ga, l1g, l1b, shf, scf, rows_per_mod, tm):
    r, d = x2.shape
    grp = lambda g: pl.BlockSpec((tm, HG_WIDTH), lambda i: (i, g))
    vec = lambda n: pl.BlockSpec((1, n), lambda i: (0, 0))
    tiles_per_mod = rows_per_mod // tm
    mod = pl.BlockSpec((1, 1, d), lambda i: (i // tiles_per_mod, 0, 0))
    row_spec = pl.BlockSpec((tm, d), lambda i: (i, 0))
    return pl.pallas_call(
        _mix_out_kernel,
        grid=(r // tm,),
        in_specs=[grp(0), grp(0), grp(4), grp(5), grp(6), grp(7), row_spec, vec(HEAD),
                  pl.BlockSpec((3, CONV_WIDTH), lambda i: (0, 0)),
                  pl.BlockSpec((d, d), lambda i: (0, 0)),
                  mod, vec(d), vec(d), mod, mod],
        out_specs=[row_spec, row_spec],
        out_shape=[jax.ShapeDtypeStruct((r, d), F32), jax.ShapeDtypeStruct((r, d), BF16)],
        scratch_shapes=[pltpu.VMEM((tm, d), BF16)],
        compiler_params=_params("parallel"),
        name="mix_out",
    )(of2, ob2, proj2, proj2, proj2, proj2, x2, gnw, cw, wo, ga, l1g, l1b, shf, scf)


def _ffn_kernel(x1_ref, h2_ref, wg_ref, wu_ref, wd_ref, ga_ref, l2g_ref, l2b_ref, o_ref, acc_ref):
    j = pl.program_id(1)
    tm = x1_ref.shape[0]
    h2 = h2_ref[...]
    g = jnp.dot(h2, wg_ref[...], preferred_element_type=F32)
    u = jnp.dot(h2, wu_ref[...], preferred_element_type=F32)
    a = (g * _sigmoid(g) * u).astype(BF16)
    part = jnp.dot(a, wd_ref[...], preferred_element_type=F32)

    @pl.when(j == 0)
    def _():
        acc_ref[...] = part

    @pl.when(j > 0)
    def _():
        acc_ref[...] += part

    @pl.when(j == pl.num_programs(1) - 1)
    def _():
        ga = ga_ref[0]
        l2g = l2g_ref[...]
        l2b = l2b_ref[...]

        def body(r, carry):
            rows = pl.ds(pl.multiple_of(r * ROWS, ROWS), ROWS)
            o_ref[rows, :] = _ln(ALPHA * x1_ref[rows, :] + ga * acc_ref[rows, :]) * l2g + l2b
            return carry

        lax.fori_loop(0, tm // ROWS, body, 0)


def _ffn(x1, h2, wg, wu, wd, ga, l2g, l2b, rows_per_mod, tm, tf):
    r, d = x1.shape
    dff = wg.shape[1]
    tiles_per_mod = rows_per_mod // tm
    row_spec = pl.BlockSpec((tm, d), lambda i, j: (i, 0))
    vec = pl.BlockSpec((1, d), lambda i, j: (0, 0))
    return pl.pallas_call(
        _ffn_kernel,
        grid=(r // tm, dff // tf),
        in_specs=[row_spec, row_spec,
                  pl.BlockSpec((d, tf), lambda i, j: (0, j)),
                  pl.BlockSpec((d, tf), lambda i, j: (0, j)),
                  pl.BlockSpec((tf, d), lambda i, j: (j, 0)),
                  pl.BlockSpec((1, 1, d), lambda i, j: (i // tiles_per_mod, 0, 0)),
                  vec, vec],
        out_specs=row_spec,
        out_shape=jax.ShapeDtypeStruct((r, d), F32),
        scratch_shapes=[pltpu.VMEM((tm, d), F32)],
        compiler_params=_params("parallel", "arbitrary"),
        name="ffn",
    )(x1, h2, wg, wu, wd, ga, l2g, l2b)


def kernel(x, c, ctx, c_ctx, w_mod, b_mod, w_in, lb_logits, g_norm_w, conv_w, w_out,
           ln1_g, ln1_b, w_gate, w_up, w_down, ln2_g, ln2_b):
    bsz, n, d = x.shape
    n_ctx = ctx.shape[1]
    assert w_mod.shape[0] == DEPTH and lb_logits.shape[1] == DEPTH + 1
    assert n % GRID_W == 0 and n_ctx % CHUNK == 0

    a = jnp.concatenate([c, c_ctx[None, :], jnp.zeros((8 - bsz - 1, d), F32)], axis=0)
    mod = _mod(a, w_mod[0], b_mod[0][None, :]).reshape(8, 6, 1, d)
    sh_a, sc_a, ga_a, sh_f, sc_f, ga_f = (mod[:bsz, i] for i in range(6))
    shc_a, scc_a = mod[bsz:bsz + 1, 0], mod[bsz:bsz + 1, 1]

    w_in_b = w_in[0].astype(BF16)
    lbl = lb_logits.reshape(2 * (DEPTH + 1), HG_WIDTH)

    xc2 = ctx.reshape(bsz * n_ctx, d)
    pc = _ln_matmul(xc2, shc_a, scc_a, w_in_b[:, :3 * HG_WIDTH], bsz * n_ctx,
                    min(512, bsz * n_ctx), 1024, F32)
    sf0, sb0 = _ctx_state(pc.reshape(bsz, n_ctx, 3 * HG_WIDTH), lbl)

    x2 = x.reshape(bsz * n, d)
    proj = _ln_matmul(x2, sh_a, sc_a, w_in_b, n, min(1024, n), 1024, F32)
    o_f, o_b = _scan(proj.reshape(bsz, n, -1), lbl, sf0, sb0, min(1024, n), 2)
    x1, h2 = _mix_out(o_f.reshape(bsz * n, HG_WIDTH), o_b.reshape(bsz * n, HG_WIDTH), proj, x2,
                      g_norm_w[0][None, :], conv_w[0], w_out[0].astype(BF16), ga_a,
                      ln1_g[0][None, :], ln1_b[0][None, :], sh_f, sc_f, n, min(256, n))
    out = _ffn(x1, h2, w_gate[0].astype(BF16), w_up[0].astype(BF16), w_down[0].astype(BF16),
               ga_f, ln2_g[0][None, :], ln2_b[0][None, :], n, min(512, n), 512)
    return out.reshape(bsz, n, d)
```

```python
import functools

import jax
import jax.numpy as jnp
from jax import lax
from jax.experimental import pallas as pl
from jax.experimental.pallas import tpu as pltpu

HEAD = 128
HG_WIDTH = 1024
N_HEADS = HG_WIDTH // HEAD
CONV_WIDTH = 1024
GRID_W = 64
DEPTH = 1
ALPHA = (2.0 * DEPTH) ** 0.25
LN_EPS = 1e-6
RMS_EPS = 1e-6

CHUNK = 64
EXP_CLAMP = 80.0
ROWS = 64
VMEM_LIMIT = 56 * 1024 * 1024

F32 = jnp.float32
BF16 = jnp.bfloat16
HIGHEST = lax.Precision.HIGHEST


def _sigmoid(x):
    return 1.0 / (1.0 + jnp.exp(-x))


def _ln(x):
    mu = jnp.mean(x, axis=-1, keepdims=True)
    xc = x - mu
    var = jnp.mean(xc * xc, axis=-1, keepdims=True)
    return xc * lax.rsqrt(var + LN_EPS)


def _params(*sem):
    return pltpu.CompilerParams(dimension_semantics=sem, vmem_limit_bytes=VMEM_LIMIT)


def _mod_kernel(a_ref, w_ref, b_ref, o_ref):
    a = a_ref[...]
    a = a * _sigmoid(a)
    o_ref[...] = jnp.dot(a, w_ref[...], precision=HIGHEST, preferred_element_type=F32) + b_ref[...]


def _mod(a, w, b):
    d, n = w.shape
    tn = 1024
    return pl.pallas_call(
        _mod_kernel,
        grid=(n // tn,),
        in_specs=[
            pl.BlockSpec((8, d), lambda j: (0, 0)),
            pl.BlockSpec((d, tn), lambda j: (0, j)),
            pl.BlockSpec((1, tn), lambda j: (0, j)),
        ],
        out_specs=pl.BlockSpec((8, tn), lambda j: (0, j)),
        out_shape=jax.ShapeDtypeStruct((8, n), F32),
        compiler_params=_params("parallel"),
        name="mod",
    )(a, w, b)


def _ln_matmul_kernel(x_ref, sh_ref, sc_ref, w_ref, o_ref, h_ref):
    tm = x_ref.shape[0]

    @pl.when(pl.program_id(1) == 0)
    def _():
        scale = 1.0 + sc_ref[0]
        shift = sh_ref[0]

        def body(r, carry):
            rows = pl.ds(pl.multiple_of(r * ROWS, ROWS), ROWS)
            h_ref[rows, :] = (_ln(x_ref[rows, :]) * scale + shift).astype(BF16)
            return carry

        lax.fori_loop(0, tm // ROWS, body, 0)

    o_ref[...] = jnp.dot(h_ref[...], w_ref[...], preferred_element_type=F32).astype(o_ref.dtype)


def _ln_matmul(x2, shift, scale, w, rows_per_mod, tm, tn, out_dtype):
    r, d = x2.shape
    n = w.shape[1]
    tiles_per_mod = rows_per_mod // tm
    mod_map = lambda i, j: (i // tiles_per_mod, 0, 0)
    return pl.pallas_call(
        _ln_matmul_kernel,
        grid=(r // tm, n // tn),
        in_specs=[
            pl.BlockSpec((tm, d), lambda i, j: (i, 0)),
            pl.BlockSpec((1, 1, d), mod_map),
            pl.BlockSpec((1, 1, d), mod_map),
            pl.BlockSpec((d, tn), lambda i, j: (0, j)),
        ],
        out_specs=pl.BlockSpec((tm, tn), lambda i, j: (i, j)),
        out_shape=jax.ShapeDtypeStruct((r, n), out_dtype),
        scratch_shapes=[pltpu.VMEM((tm, d), BF16)],
        compiler_params=_params("parallel", "arbitrary"),
        name="ln_matmul",
    )(x2, shift, scale, w)


def _lower_bound(lbl, direction):
    return _sigmoid(lbl[2 * direction:2 * direction + 1, :] - lbl[2 * direction + 1:2 * direction + 2, :])


def _forget(f_raw, lb):
    f = lb + (1.0 - lb) * _sigmoid(f_raw)
    return 1.0 - f, jnp.log(f)


def _tri(n, kind):
    r = lax.broadcasted_iota(jnp.int32, (n, n), 0)
    c = lax.broadcasted_iota(jnp.int32, (n, n), 1)
    cond = {"le": c <= r, "ge": c >= r, "lt": c < r, "gt": c > r}[kind]
    return cond


def _ctx_state_kernel(pf_ref, pb_ref, pi_ref, lbl_ref, sf_ref, sb_ref):
    n = pf_ref.shape[1]
    lbl = lbl_ref[...]
    v = pi_ref[0]
    for d, (p_ref, kind, s_ref) in enumerate(((pf_ref, "gt", sf_ref), (pb_ref, "lt", sb_ref))):
        k, lg = _forget(p_ref[0], _lower_bound(lbl, d))
        tri = _tri(n, kind).astype(F32)
        w = jnp.exp(jnp.dot(tri, lg, precision=HIGHEST, preferred_element_type=F32))
        s_ref[0, 0] = lax.dot_general(v, k * w, (((0,), (0,)), ((), ())), precision=HIGHEST,
                                      preferred_element_type=F32)


def _ctx_state(pc3, lbl):
    b, n, _ = pc3.shape
    hb = HG_WIDTH // HEAD
    spec = lambda off: pl.BlockSpec((1, n, HEAD), lambda bi, h: (bi, 0, off * hb + h))
    s_spec = pl.BlockSpec((1, 1, HEAD, HEAD), lambda bi, h: (bi, h, 0, 0))
    s_shape = jax.ShapeDtypeStruct((b, N_HEADS, HEAD, HEAD), F32)
    return pl.pallas_call(
        _ctx_state_kernel,
        grid=(b, N_HEADS),
        in_specs=[spec(0), spec(1), spec(2), pl.BlockSpec((4, HEAD), lambda bi, h: (0, h))],
        out_specs=[s_spec, s_spec],
        out_shape=[s_shape, s_shape],
        compiler_params=_params("parallel", "parallel"),
        name="ctx_state",
    )(pc3, pc3, pc3, lbl)


def _cumsum_rows(tri, x):
    hi = x.astype(BF16)
    r1 = x - hi.astype(F32)
    mid = r1.astype(BF16)
    lo = (r1 - mid.astype(F32)).astype(BF16)
    dot = lambda part: jnp.dot(tri, part, preferred_element_type=F32)
    return dot(hi) + dot(mid) + dot(lo)


def _scan_prep(f_ref, i_ref, q_ref, rows, lb, tri, end_row):
    k, lg = _forget(f_ref[0, rows, :], lb)
    q_raw = q_ref[0, rows, :]
    q = q_raw * _sigmoid(q_raw)
    b = _cumsum_rows(tri, lg)
    b_end = b[end_row:end_row + 1, :]
    m = 0.5 * b_end
    qt = (q * jnp.exp(jnp.minimum(b - m, EXP_CLAMP))).astype(BF16)
    kt = (k * jnp.exp(jnp.minimum(m - b, EXP_CLAMP))).astype(BF16)
    kp = (k * jnp.exp(b_end - b)).astype(BF16)
    return qt, kt, kp, jnp.exp(m), jnp.exp(b_end), i_ref[0, rows, :].astype(BF16)


def _scan_step(dirs):
    units = [(d, h) for d in range(len(dirs)) for h in range(dirs[d][1].shape[0])]
    cols = lambda h: slice(h * HEAD, (h + 1) * HEAD)
    scores = {}
    for d, h in units:
        (qt, kt, _, em, _, _), st_ref = dirs[d][:2]
        w = jnp.concatenate([(st_ref[h] * em[:, cols(h)]).astype(BF16), kt[:, cols(h)]], axis=0)
        scores[d, h] = lax.dot_general(qt[:, cols(h)], w, (((1,), (1,)), ((), ())),
                                       preferred_element_type=F32)
    for d, h in units:
        (_, _, kp, _, eb, vb), st_ref = dirs[d][:2]
        st_ref[h] = st_ref[h] * eb[:, cols(h)] + lax.dot_general(
            vb[:, cols(h)], kp[:, cols(h)], (((0,), (0,)), ((), ())), preferred_element_type=F32)
    for d, h in units:
        prep, _, o_ref, rows, mask = dirs[d]
        a = scores[d, h]
        p = jnp.where(mask, a[:, HEAD:], 0.0).astype(BF16)
        o_ref[0, rows, cols(h)] = a[:, :HEAD] + jnp.dot(p, prep[5][:, cols(h)],
                                                         preferred_element_type=F32)


def _scan_kernel(ff_ref, if_ref, qf_ref, fb_ref, ib_ref, qb_ref, lbl_ref, sf0_ref, sb0_ref,
                 of_ref, ob_ref, stf_ref, stb_ref):
    nc = ff_ref.shape[1] // CHUNK

    @pl.when(pl.program_id(2) == 0)
    def _():
        stf_ref[...] = sf0_ref[0]
        stb_ref[...] = sb0_ref[0]

    lbl = lbl_ref[...]
    lb_f = _lower_bound(lbl, 0)
    lb_b = _lower_bound(lbl, 1)
    mask_f = _tri(CHUNK, "le")
    mask_b = _tri(CHUNK, "ge")
    tri_f = mask_f.astype(BF16)
    tri_b = mask_b.astype(BF16)

    def body(j, carry):
        rf = pl.ds(pl.multiple_of(j * CHUNK, CHUNK), CHUNK)
        rb = pl.ds(pl.multiple_of((nc - 1 - j) * CHUNK, CHUNK), CHUNK)
        prep_f = _scan_prep(ff_ref, if_ref, qf_ref, rf, lb_f, tri_f, CHUNK - 1)
        prep_b = _scan_prep(fb_ref, ib_ref, qb_ref, rb, lb_b, tri_b, 0)
        _scan_step([(prep_f, stf_ref, of_ref, rf, mask_f), (prep_b, stb_ref, ob_ref, rb, mask_b)])
        return carry

    lax.fori_loop(0, nc, body, 0)


def _scan(proj3, lbl, sf0, sb0, t, hb):
    b, n, _ = proj3.shape
    nt = n // t
    w = hb * HEAD
    grp = HG_WIDTH // w
    fwd = lambda g: pl.BlockSpec((1, t, w), lambda bi, h, c: (bi, c, g * grp + h))
    bwd = lambda g: pl.BlockSpec((1, t, w), lambda bi, h, c: (bi, nt - 1 - c, g * grp + h))
    s_spec = pl.BlockSpec((1, hb, HEAD, HEAD), lambda bi, h, c: (bi, h, 0, 0))
    o_shape = jax.ShapeDtypeStruct((b, n, HG_WIDTH), F32)
    return pl.pallas_call(
        _scan_kernel,
        grid=(b, N_HEADS // hb, nt),
        in_specs=[fwd(0), fwd(2), fwd(3), bwd(1), bwd(2), bwd(3),
                  pl.BlockSpec((4, w), lambda bi, h, c: (0, h)), s_spec, s_spec],
        out_specs=[pl.BlockSpec((1, t, w), lambda bi, h, c: (bi, c, h)),
                   pl.BlockSpec((1, t, w), lambda bi, h, c: (bi, nt - 1 - c, h))],
        out_shape=[o_shape, o_shape],
        scratch_shapes=[pltpu.VMEM((hb, HEAD, HEAD), F32), pltpu.VMEM((hb, HEAD, HEAD), F32)],
        compiler_params=_params("parallel", "parallel", "arbitrary"),
        name="scan",
    )(proj3, proj3, proj3, proj3, proj3, proj3, lbl, sf0, sb0)


def _mix_out_kernel(of_ref, ob_ref, g_ref, bg_ref, cg_ref, xv_ref, x_ref, gnw_ref, cw_ref, wo_ref,
                    ga_ref, l1g_ref, l1b_ref, shf_ref, scf_ref, x1_ref, h2_ref, y_ref):
    tm = x_ref.shape[0]
    gnw = gnw_ref[...]
    cw = cw_ref[...]
    row = lax.broadcasted_iota(jnp.int32, (GRID_W, 1), 0)
    first = row == 0
    last = row == GRID_W - 1

    def prologue(r, carry):
        rows = pl.ds(pl.multiple_of(r * GRID_W, GRID_W), GRID_W)
        for h in range(N_HEADS):
            cols = slice(h * HEAD, (h + 1) * HEAD)
            o = of_ref[rows, cols] + ob_ref[rows, cols]
            ms = jnp.mean(o * o, axis=-1, keepdims=True)
            g = g_ref[rows, cols]
            y_ref[rows, cols] = (o * lax.rsqrt(ms + RMS_EPS) * gnw * (g * _sigmoid(g))).astype(BF16)
        z = cg_ref[rows, :] * xv_ref[rows, :]
        z_prev = jnp.where(first, 0.0, pltpu.roll(z, 1, 0))
        z_next = jnp.where(last, 0.0, pltpu.roll(z, GRID_W - 1, 0))
        conv = cw[0:1, :] * z_prev + cw[1:2, :] * z + cw[2:3, :] * z_next
        y_ref[rows, HG_WIDTH:] = (bg_ref[rows, :] * conv).astype(BF16)
        return carry

    lax.fori_loop(0, tm // GRID_W, prologue, 0)

    y_ref_f32 = jnp.dot(y_ref[...], wo_ref[...], preferred_element_type=F32)
    x1_ref[...] = y_ref_f32

    ga = ga_ref[0]
    l1g = l1g_ref[...]
    l1b = l1b_ref[...]
    scale = 1.0 + scf_ref[0]
    shift = shf_ref[0]

    def epilogue(r, carry):
        rows = pl.ds(pl.multiple_of(r * ROWS, ROWS), ROWS)
        x1 = _ln(ALPHA * x_ref[rows, :] + ga * x1_ref[rows, :]) * l1g + l1b
        x1_ref[rows, :] = x1
        h2_ref[rows, :] = (_ln(x1) * scale + shift).astype(BF16)
        return carry

    lax.fori_loop(0, tm // ROWS, epilogue, 0)


def _mix_out(of2, ob2, proj2, x2, gnw, cw, wo, ga, l1g, l1b, shf, scf, rows_per_mod, tm):
    r, d = x2.shape
    grp = lambda g: pl.BlockSpec((tm, HG_WIDTH), lambda i: (i, g))
    vec = lambda n: pl.BlockSpec((1, n), lambda i: (0, 0))
    tiles_per_mod = rows_per_mod // tm
    mod = pl.BlockSpec((1, 1, d), lambda i: (i // tiles_per_mod, 0, 0))
    row_spec = pl.BlockSpec((tm, d), lambda i: (i, 0))
    return pl.pallas_call(
        _mix_out_kernel,
        grid=(r // tm,),
        in_specs=[grp(0), grp(0), grp(4), grp(5), grp(6), grp(7), row_spec, vec(HEAD),
                  pl.BlockSpec((3, CONV_WIDTH), lambda i: (0, 0)),
                  pl.BlockSpec((d, d), lambda i: (0, 0)),
                  mod, vec(d), vec(d), mod, mod],
        out_specs=[row_spec, row_spec],
        out_shape=[jax.ShapeDtypeStruct((r, d), F32), jax.ShapeDtypeStruct((r, d), BF16)],
        scratch_shapes=[pltpu.VMEM((tm, d), BF16)],
        compiler_params=_params("parallel"),
        name="mix_out",
    )(of2, ob2, proj2, proj2, proj2, proj2, x2, gnw, cw, wo, ga, l1g, l1b, shf, scf)


def _ffn_kernel(x1_ref, h2_ref, wg_ref, wu_ref, wd_ref, ga_ref, l2g_ref, l2b_ref, o_ref, acc_ref):
    j = pl.program_id(1)
    tm = x1_ref.shape[0]
    h2 = h2_ref[...]
    g = jnp.dot(h2, wg_ref[...], preferred_element_type=F32)
    u = jnp.dot(h2, wu_ref[...], preferred_element_type=F32)
    a = (g * _sigmoid(g) * u).astype(BF16)
    part = jnp.dot(a, wd_ref[...], preferred_element_type=F32)

    @pl.when(j == 0)
    def _():
        acc_ref[...] = part

    @pl.when(j > 0)
    def _():
        acc_ref[...] += part

    @pl.when(j == pl.num_programs(1) - 1)
    def _():
        ga = ga_ref[0]
        l2g = l2g_ref[...]
        l2b = l2b_ref[...]

        def body(r, carry):
            rows = pl.ds(pl.multiple_of(r * ROWS, ROWS), ROWS)
            o_ref[rows, :] = _ln(ALPHA * x1_ref[rows, :] + ga * acc_ref[rows, :]) * l2g + l2b
            return carry

        lax.fori_loop(0, tm // ROWS, body, 0)


def _ffn(x1, h2, wg, wu, wd, ga, l2g, l2b, rows_per_mod, tm, tf):
    r, d = x1.shape
    dff = wg.shape[1]
    tiles_per_mod = rows_per_mod // tm
    row_spec = pl.BlockSpec((tm, d), lambda i, j: (i, 0))
    vec = pl.BlockSpec((1, d), lambda i, j: (0, 0))
    return pl.pallas_call(
        _ffn_kernel,
        grid=(r // tm, dff // tf),
        in_specs=[row_spec, row_spec,
                  pl.BlockSpec((d, tf), lambda i, j: (0, j)),
                  pl.BlockSpec((d, tf), lambda i, j: (0, j)),
                  pl.BlockSpec((tf, d), lambda i, j: (j, 0)),
                  pl.BlockSpec((1, 1, d), lambda i, j: (i // tiles_per_mod, 0, 0)),
                  vec, vec],
        out_specs=row_spec,
        out_shape=jax.ShapeDtypeStruct((r, d), F32),
        scratch_shapes=[pltpu.VMEM((tm, d), F32)],
        compiler_params=_params("parallel", "arbitrary"),
        name="ffn",
    )(x1, h2, wg, wu, wd, ga, l2g, l2b)


def kernel(x, c, ctx, c_ctx, w_mod, b_mod, w_in, lb_logits, g_norm_w, conv_w, w_out,
           ln1_g, ln1_b, w_gate, w_up, w_down, ln2_g, ln2_b):
    bsz, n, d = x.shape
    n_ctx = ctx.shape[1]
    assert w_mod.shape[0] == DEPTH and lb_logits.shape[1] == DEPTH + 1
    assert n % GRID_W == 0 and n_ctx % CHUNK == 0

    a = jnp.concatenate([c, c_ctx[None, :], jnp.zeros((8 - bsz - 1, d), F32)], axis=0)
    mod = _mod(a, w_mod[0], b_mod[0][None, :]).reshape(8, 6, 1, d)
    sh_a, sc_a, ga_a, sh_f, sc_f, ga_f = (mod[:bsz, i] for i in range(6))
    shc_a, scc_a = mod[bsz:bsz + 1, 0], mod[bsz:bsz + 1, 1]

    w_in_b = w_in[0].astype(BF16)
    lbl = lb_logits.reshape(2 * (DEPTH + 1), HG_WIDTH)

    xc2 = ctx.reshape(bsz * n_ctx, d)
    pc = _ln_matmul(xc2, shc_a, scc_a, w_in_b[:, :3 * HG_WIDTH], bsz * n_ctx,
                    min(512, bsz * n_ctx), 1024, F32)
    sf0, sb0 = _ctx_state(pc.reshape(bsz, n_ctx, 3 * HG_WIDTH), lbl)

    x2 = x.reshape(bsz * n, d)
    proj = _ln_matmul(x2, sh_a, sc_a, w_in_b, n, min(1024, n), 1024, F32)
    o_f, o_b = _scan(proj.reshape(bsz, n, -1), lbl, sf0, sb0, min(512, n), 8)
    x1, h2 = _mix_out(o_f.reshape(bsz * n, HG_WIDTH), o_b.reshape(bsz * n, HG_WIDTH), proj, x2,
                      g_norm_w[0][None, :], conv_w[0], w_out[0].astype(BF16), ga_a,
                      ln1_g[0][None, :], ln1_b[0][None, :], sh_f, sc_f, n, min(256, n))
    out = _ffn(x1, h2, w_gate[0].astype(BF16), w_up[0].astype(BF16), w_down[0].astype(BF16),
               ga_f, ln2_g[0][None, :], ln2_b[0][None, :], n, min(512, n), 512)
    return out.reshape(bsz, n, d)
```

```python
import functools

import jax
import jax.numpy as jnp
from jax import lax
from jax.experimental import pallas as pl
from jax.experimental.pallas import tpu as pltpu

HEAD = 128
HG_WIDTH = 1024
N_HEADS = HG_WIDTH // HEAD
CONV_WIDTH = 1024
GRID_W = 64
DEPTH = 1
ALPHA = (2.0 * DEPTH) ** 0.25
LN_EPS = 1e-6
RMS_EPS = 1e-6

GROUP = 1024
IN_PROJ_ROWS = 256
MIX_SUB = 128
CHUNK = 64
EXP_CLAMP = 80.0
ROWS = 64
VMEM_LIMIT = 56 * 1024 * 1024

F32 = jnp.float32
BF16 = jnp.bfloat16
HIGHEST = lax.Precision.HIGHEST


def _sigmoid(x):
    return 1.0 / (1.0 + jnp.exp(-x))


def _ln(x):
    mu = jnp.mean(x, axis=-1, keepdims=True)
    xc = x - mu
    var = jnp.mean(xc * xc, axis=-1, keepdims=True)
    return xc * lax.rsqrt(var + LN_EPS)


def _params(*sem):
    return pltpu.CompilerParams(dimension_semantics=sem, vmem_limit_bytes=VMEM_LIMIT)


def _mod_kernel(a_ref, w_ref, b_ref, o_ref):
    a = a_ref[...]
    a = a * _sigmoid(a)
    o_ref[...] = jnp.dot(a, w_ref[...], precision=HIGHEST, preferred_element_type=F32) + b_ref[...]


def _mod(a, w, b):
    d, n = w.shape
    tn = 1024
    return pl.pallas_call(
        _mod_kernel,
        grid=(n // tn,),
        in_specs=[
            pl.BlockSpec((8, d), lambda j: (0, 0)),
            pl.BlockSpec((d, tn), lambda j: (0, j)),
            pl.BlockSpec((1, tn), lambda j: (0, j)),
        ],
        out_specs=pl.BlockSpec((8, tn), lambda j: (0, j)),
        out_shape=jax.ShapeDtypeStruct((8, n), F32),
        compiler_params=_params("parallel"),
        name="mod",
    )(a, w, b)


def _lower_bound(lbl, direction):
    return _sigmoid(lbl[2 * direction:2 * direction + 1, :] - lbl[2 * direction + 1:2 * direction + 2, :])


def _in_proj_kernel(x_ref, sh_ref, sc_ref, lbl_ref, w_ref, o_ref, hcur_ref, hnext_ref):
    tm = x_ref.shape[0]

    @pl.when(pl.program_id(0) == 0)
    def _():
        hnext_ref[...] = jnp.zeros_like(hnext_ref)

    hcur_ref[...] = hnext_ref[...]
    h = hcur_ref[...]
    lbl = lbl_ref[...]
    for g in range(w_ref.shape[1] // GROUP):
        cols = slice(g * GROUP, (g + 1) * GROUP)
        acc = jnp.dot(h, w_ref[:, cols], preferred_element_type=F32)
        if g in (0, 1):
            lb = _lower_bound(lbl, g)
            acc = jnp.log(lb + (1.0 - lb) * _sigmoid(acc))
        elif g in (3, 4):
            acc = acc * _sigmoid(acc)
        o_ref[:, cols] = acc.astype(BF16)

    scale = 1.0 + sc_ref[0]
    shift = sh_ref[0]
    for r in range(tm // ROWS):
        rows = slice(r * ROWS, (r + 1) * ROWS)
        hnext_ref[rows, :] = (_ln(x_ref[rows, :]) * scale + shift).astype(BF16)


def _in_proj(x2, shift, scale, lbl, w, rows_per_mod, tm):
    r, d = x2.shape
    n = w.shape[1]
    nt = r // tm
    tiles_per_mod = rows_per_mod // tm
    mod_map = lambda s: (jnp.minimum(s, nt - 1) // tiles_per_mod, 0, 0)
    return pl.pallas_call(
        _in_proj_kernel,
        grid=(nt + 1,),
        in_specs=[
            pl.BlockSpec((tm, d), lambda s: (jnp.minimum(s, nt - 1), 0)),
            pl.BlockSpec((1, 1, d), mod_map),
            pl.BlockSpec((1, 1, d), mod_map),
            pl.BlockSpec(lbl.shape, lambda s: (0, 0)),
            pl.BlockSpec((d, n), lambda s: (0, 0), pipeline_mode=pl.Buffered(1)),
        ],
        out_specs=pl.BlockSpec((tm, n), lambda s: (jnp.maximum(s - 1, 0), 0)),
        out_shape=jax.ShapeDtypeStruct((r, n), BF16),
        scratch_shapes=[pltpu.VMEM((tm, d), BF16), pltpu.VMEM((tm, d), BF16)],
        compiler_params=_params("arbitrary"),
        name="in_proj",
    )(x2, shift, scale, lbl, w)


def _tri(n, kind):
    r = lax.broadcasted_iota(jnp.int32, (n, n), 0)
    c = lax.broadcasted_iota(jnp.int32, (n, n), 1)
    cond = {"le": c <= r, "ge": c >= r, "lt": c < r, "gt": c > r}[kind]
    return cond


def _ctx_state_kernel(pf_ref, pb_ref, pi_ref, sf_ref, sb_ref):
    n = pf_ref.shape[1]
    v = pi_ref[0]
    for p_ref, kind, s_ref in ((pf_ref, "gt", sf_ref), (pb_ref, "lt", sb_ref)):
        lg = p_ref[0]
        k = 1.0 - jnp.exp(lg.astype(F32))
        w = jnp.exp(jnp.dot(_tri(n, kind).astype(BF16), lg, preferred_element_type=F32))
        s_ref[0, 0] = lax.dot_general(v, (k * w).astype(BF16), (((0,), (0,)), ((), ())),
                                      preferred_element_type=F32)


def _ctx_state(pc3):
    b, n, _ = pc3.shape
    hb = HG_WIDTH // HEAD
    spec = lambda off: pl.BlockSpec((1, n, HEAD), lambda bi, h: (bi, 0, off * hb + h))
    s_spec = pl.BlockSpec((1, 1, HEAD, HEAD), lambda bi, h: (bi, h, 0, 0))
    s_shape = jax.ShapeDtypeStruct((b, N_HEADS, HEAD, HEAD), F32)
    return pl.pallas_call(
        _ctx_state_kernel,
        grid=(b, N_HEADS),
        in_specs=[spec(0), spec(1), spec(2)],
        out_specs=[s_spec, s_spec],
        out_shape=[s_shape, s_shape],
        compiler_params=_params("parallel", "parallel"),
        name="ctx_state",
    )(pc3, pc3, pc3)


def _scan_prep(lg_ref, i_ref, q_ref, rows, tri, end_row):
    lg = lg_ref[0, rows, :]
    k = 1.0 - jnp.exp(lg.astype(F32))
    q = q_ref[0, rows, :].astype(F32)
    b = jnp.dot(tri, lg, preferred_element_type=F32)
    b_end = b[end_row:end_row + 1, :]
    m = 0.5 * b_end
    qt = (q * jnp.exp(jnp.minimum(b - m, EXP_CLAMP))).astype(BF16)
    kt = (k * jnp.exp(jnp.minimum(m - b, EXP_CLAMP))).astype(BF16)
    kp = (k * jnp.exp(b_end - b)).astype(BF16)
    return qt, kt, kp, jnp.exp(m), jnp.exp(b_end), i_ref[0, rows, :]


def _scan_step(dirs):
    units = [(d, h) for d in range(len(dirs)) for h in range(dirs[d][1].shape[0])]
    cols = lambda h: slice(h * HEAD, (h + 1) * HEAD)
    scores = {}
    for d, h in units:
        (qt, kt, _, em, _, _), st_ref = dirs[d][:2]
        w = jnp.concatenate([(st_ref[h] * em[:, cols(h)]).astype(BF16), kt[:, cols(h)]], axis=0)
        scores[d, h] = lax.dot_general(qt[:, cols(h)], w, (((1,), (1,)), ((), ())),
                                       preferred_element_type=F32)
    for d, h in units:
        (_, _, kp, _, eb, vb), st_ref = dirs[d][:2]
        st_ref[h] = st_ref[h] * eb[:, cols(h)] + lax.dot_general(
            vb[:, cols(h)], kp[:, cols(h)], (((0,), (0,)), ((), ())), preferred_element_type=F32)
    for d, h in units:
        prep, _, o_ref, rows, mask = dirs[d]
        a = scores[d, h]
        p = jnp.where(mask, a[:, HEAD:], 0.0).astype(BF16)
        o_ref[0, rows, cols(h)] = (a[:, :HEAD] + jnp.dot(p, prep[5][:, cols(h)],
                                                          preferred_element_type=F32)).astype(o_ref.dtype)


def _scan_kernel(ff_ref, if_ref, qf_ref, fb_ref, ib_ref, qb_ref, sf0_ref, sb0_ref,
                 of_ref, ob_ref, stf_ref, stb_ref):
    nc = ff_ref.shape[1] // CHUNK

    @pl.when(pl.program_id(2) == 0)
    def _():
        stf_ref[...] = sf0_ref[0]
        stb_ref[...] = sb0_ref[0]

    mask_f = _tri(CHUNK, "le")
    mask_b = _tri(CHUNK, "ge")
    tri_f = mask_f.astype(BF16)
    tri_b = mask_b.astype(BF16)

    def body(j, carry):
        rf = pl.ds(pl.multiple_of(j * CHUNK, CHUNK), CHUNK)
        rb = pl.ds(pl.multiple_of((nc - 1 - j) * CHUNK, CHUNK), CHUNK)
        prep_f = _scan_prep(ff_ref, if_ref, qf_ref, rf, tri_f, CHUNK - 1)
        prep_b = _scan_prep(fb_ref, ib_ref, qb_ref, rb, tri_b, 0)
        _scan_step([(prep_f, stf_ref, of_ref, rf, mask_f), (prep_b, stb_ref, ob_ref, rb, mask_b)])
        return carry

    lax.fori_loop(0, nc, body, 0)


def _scan(proj3, sf0, sb0, t, hb):
    b, n, _ = proj3.shape
    nt = n // t
    w = hb * HEAD
    grp = HG_WIDTH // w
    fwd = lambda g: pl.BlockSpec((1, t, w), lambda bi, h, c: (bi, c, g * grp + h))
    bwd = lambda g: pl.BlockSpec((1, t, w), lambda bi, h, c: (bi, nt - 1 - c, g * grp + h))
    s_spec = pl.BlockSpec((1, hb, HEAD, HEAD), lambda bi, h, c: (bi, h, 0, 0))
    o_shape = jax.ShapeDtypeStruct((b, n, HG_WIDTH), BF16)
    return pl.pallas_call(
        _scan_kernel,
        grid=(b, N_HEADS // hb, nt),
        in_specs=[fwd(0), fwd(2), fwd(3), bwd(1), bwd(2), bwd(3), s_spec, s_spec],
        out_specs=[pl.BlockSpec((1, t, w), lambda bi, h, c: (bi, c, h)),
                   pl.BlockSpec((1, t, w), lambda bi, h, c: (bi, nt - 1 - c, h))],
        out_shape=[o_shape, o_shape],
        scratch_shapes=[pltpu.VMEM((hb, HEAD, HEAD), F32), pltpu.VMEM((hb, HEAD, HEAD), F32)],
        compiler_params=_params("parallel", "parallel", "arbitrary"),
        name="scan",
    )(proj3, proj3, proj3, proj3, proj3, proj3, sf0, sb0)


def _mix_out_kernel(of_ref, ob_ref, g_ref, bg_ref, cg_ref, xv_ref, x_ref, gnw_ref, cw_ref, wo_ref,
                    ga_ref, l1g_ref, l1b_ref, shf_ref, scf_ref, x1_ref, h2_ref):
    tm = x_ref.shape[0]
    gnw = gnw_ref[...]
    cw = cw_ref[...]
    row = lax.broadcasted_iota(jnp.int32, (GRID_W, 1), 0)
    first = row == 0
    last = row == GRID_W - 1
    ga = ga_ref[0]
    l1g = l1g_ref[...]
    l1b = l1b_ref[...]
    scale = 1.0 + scf_ref[0]
    shift = shf_ref[0]

    def mixer_rows(rows):
        parts = []
        for h in range(N_HEADS):
            cols = slice(h * HEAD, (h + 1) * HEAD)
            o = of_ref[rows, cols].astype(F32) + ob_ref[rows, cols].astype(F32)
            ms = jnp.mean(o * o, axis=-1, keepdims=True)
            parts.append((o * lax.rsqrt(ms + RMS_EPS) * gnw * g_ref[rows, cols].astype(F32)).astype(BF16))
        z = cg_ref[rows, :].astype(F32) * xv_ref[rows, :].astype(F32)
        z_prev = jnp.where(first, 0.0, pltpu.roll(z, 1, 0))
        z_next = jnp.where(last, 0.0, pltpu.roll(z, GRID_W - 1, 0))
        conv = cw[0:1, :] * z_prev + cw[1:2, :] * z + cw[2:3, :] * z_next
        parts.append((bg_ref[rows, :].astype(F32) * conv).astype(BF16))
        return jnp.concatenate(parts, axis=1)

    for s in range(tm // MIX_SUB):
        base = s * MIX_SUB
        y = jnp.concatenate([mixer_rows(slice(base + r, base + r + GRID_W))
                             for r in range(0, MIX_SUB, GRID_W)], axis=0)
        res = jnp.dot(y, wo_ref[...], preferred_element_type=F32)
        for r in range(0, MIX_SUB, ROWS):
            rows = slice(base + r, base + r + ROWS)
            x1 = _ln(ALPHA * x_ref[rows, :] + ga * res[r:r + ROWS, :]) * l1g + l1b
            x1_ref[rows, :] = x1
            h2_ref[rows, :] = (_ln(x1) * scale + shift).astype(BF16)


def _mix_out(of2, ob2, proj2, x2, gnw, cw, wo, ga, l1g, l1b, shf, scf, rows_per_mod, tm):
    r, d = x2.shape
    grp = lambda g: pl.BlockSpec((tm, HG_WIDTH), lambda i: (i, g))
    vec = lambda n: pl.BlockSpec((1, n), lambda i: (0, 0))
    tiles_per_mod = rows_per_mod // tm
    mod = pl.BlockSpec((1, 1, d), lambda i: (i // tiles_per_mod, 0, 0))
    row_spec = pl.BlockSpec((tm, d), lambda i: (i, 0))
    return pl.pallas_call(
        _mix_out_kernel,
        grid=(r // tm,),
        in_specs=[grp(0), grp(0), grp(4), grp(5), grp(6), grp(7), row_spec, vec(HEAD),
                  pl.BlockSpec((3, CONV_WIDTH), lambda i: (0, 0)),
                  pl.BlockSpec((d, d), lambda i: (0, 0), pipeline_mode=pl.Buffered(1)),
                  mod, vec(d), vec(d), mod, mod],
        out_specs=[row_spec, row_spec],
        out_shape=[jax.ShapeDtypeStruct((r, d), F32), jax.ShapeDtypeStruct((r, d), BF16)],
        compiler_params=_params("parallel"),
        name="mix_out",
    )(of2, ob2, proj2, proj2, proj2, proj2, x2, gnw, cw, wo, ga, l1g, l1b, shf, scf)


def _ffn_kernel(x1_ref, h2_ref, wg_ref, wu_ref, wd_ref, ga_ref, l2g_ref, l2b_ref, o_ref, acc_ref):
    j = pl.program_id(1)
    tm = x1_ref.shape[0]

    @pl.when(j == 0)
    def _():
        acc_ref[...] = jnp.zeros_like(acc_ref)

    h2 = h2_ref[...]
    g = jnp.dot(h2, wg_ref[...], preferred_element_type=F32)
    u = jnp.dot(h2, wu_ref[...], preferred_element_type=F32)
    a = (g * _sigmoid(g) * u).astype(BF16)
    acc_ref[...] += jnp.dot(a, wd_ref[...], preferred_element_type=F32)

    @pl.when(j == pl.num_programs(1) - 1)
    def _():
        ga = ga_ref[0]
        l2g = l2g_ref[...]
        l2b = l2b_ref[...]

        def body(r, carry):
            rows = pl.ds(pl.multiple_of(r * ROWS, ROWS), ROWS)
            o_ref[rows, :] = _ln(ALPHA * x1_ref[rows, :] + ga * acc_ref[rows, :]) * l2g + l2b
            return carry

        lax.fori_loop(0, tm // ROWS, body, 0)


def _ffn(x1, h2, wg, wu, wd, ga, l2g, l2b, rows_per_mod, tm, tf):
    r, d = x1.shape
    dff = wg.shape[1]
    tiles_per_mod = rows_per_mod // tm
    row_spec = pl.BlockSpec((tm, d), lambda i, j: (i, 0))
    vec = pl.BlockSpec((1, d), lambda i, j: (0, 0))
    return pl.pallas_call(
        _ffn_kernel,
        grid=(r // tm, dff // tf),
        in_specs=[row_spec, row_spec,
                  pl.BlockSpec((d, tf), lambda i, j: (0, j)),
                  pl.BlockSpec((d, tf), lambda i, j: (0, j)),
                  pl.BlockSpec((tf, d), lambda i, j: (j, 0)),
                  pl.BlockSpec((1, 1, d), lambda i, j: (i // tiles_per_mod, 0, 0)),
                  vec, vec],
        out_specs=row_spec,
        out_shape=jax.ShapeDtypeStruct((r, d), F32),
        scratch_shapes=[pltpu.VMEM((tm, d), F32)],
        compiler_params=_params("parallel", "arbitrary"),
        name="ffn",
    )(x1, h2, wg, wu, wd, ga, l2g, l2b)


def kernel(x, c, ctx, c_ctx, w_mod, b_mod, w_in, lb_logits, g_norm_w, conv_w, w_out,
           ln1_g, ln1_b, w_gate, w_up, w_down, ln2_g, ln2_b):
    bsz, n, d = x.shape
    n_ctx = ctx.shape[1]
    assert w_mod.shape[0] == DEPTH and lb_logits.shape[1] == DEPTH + 1
    assert n % GRID_W == 0 and n_ctx % CHUNK == 0

    a = jnp.concatenate([c, c_ctx[None, :], jnp.zeros((8 - bsz - 1, d), F32)], axis=0)
    mod = _mod(a, w_mod[0], b_mod[0][None, :]).reshape(8, 6, 1, d)
    sh_a, sc_a, ga_a, sh_f, sc_f, ga_f = (mod[:bsz, i] for i in range(6))
    shc_a, scc_a = mod[bsz:bsz + 1, 0], mod[bsz:bsz + 1, 1]

    w_in_b = w_in[0].astype(BF16)
    lbl = lb_logits.reshape(2 * (DEPTH + 1), HG_WIDTH)

    xc2 = ctx.reshape(bsz * n_ctx, d)
    pc = _in_proj(xc2, shc_a, scc_a, lbl, w_in_b[:, :3 * GROUP], bsz * n_ctx, IN_PROJ_ROWS)
    sf0, sb0 = _ctx_state(pc.reshape(bsz, n_ctx, 3 * GROUP))

    x2 = x.reshape(bsz * n, d)
    proj = _in_proj(x2, sh_a, sc_a, lbl, w_in_b, n, IN_PROJ_ROWS)
    o_f, o_b = _scan(proj.reshape(bsz, n, -1), sf0, sb0, min(512, n), 8)
    x1, h2 = _mix_out(o_f.reshape(bsz * n, HG_WIDTH), o_b.reshape(bsz * n, HG_WIDTH), proj, x2,
                      g_norm_w[0][None, :], conv_w[0], w_out[0].astype(BF16), ga_a,
                      ln1_g[0][None, :], ln1_b[0][None, :], sh_f, sc_f, n, min(512, n))
    out = _ffn(x1, h2, w_gate[0].astype(BF16), w_up[0].astype(BF16), w_down[0].astype(BF16),
               ga_f, ln2_g[0][None, :], ln2_b[0][None, :], n, min(512, n), 512)
    return out.reshape(bsz, n, d)
```
